```python
import math
import jax, jax.numpy as jnp
from jax import lax
import numpy as np

D_MODEL = 2048
BATCH = 1
SEQ = 8192
DEPTH = 1

NSA_HEADS = 16
NSA_KV_HEADS = 4
NSA_Q_PER_KV = NSA_HEADS // NSA_KV_HEADS
NSA_HEAD_DIM = 128
CMP_BLOCK = 32
CMP_STRIDE = 16
SLC_BLOCK = 64
SLC_TOP = 16
WINDOW = 512
Q_BLOCK = 128
ROPE_THETA = 500000.0
ROPE_DIM = NSA_HEAD_DIM // 4
SSM_D_INNER = D_MODEL
SSM_HEAD_DIM = 64
SSM_HEADS = SSM_D_INNER // SSM_HEAD_DIM
SSM_GROUPS = 4
SSM_HEADS_PER_GROUP = SSM_HEADS // SSM_GROUPS
SSM_STATE = 128
SSM_CONV = 4
SSM_CHUNK = 128
SSM_CONV_CH = SSM_D_INNER + 2 * SSM_GROUPS * SSM_STATE
PEER_HEADS = 8
PEER_N_KEYS = 128
PEER_N_EXPERTS = PEER_N_KEYS * PEER_N_KEYS
PEER_D_KEY = 256
PEER_TOPK = 16
TOKEN_BLOCK = 128
Q_COLS = NSA_HEADS * NSA_HEAD_DIM
KV_COLS = 3 * 2 * NSA_KV_HEADS * NSA_HEAD_DIM
NSA_GATE_COLS = 3 * NSA_HEADS
Z_COLS = SSM_D_INNER
XBC_COLS = SSM_CONV_CH
DT_COLS = SSM_HEADS
MERGE_COLS = 2 * D_MODEL
IN_SIZES = (Q_COLS, KV_COLS, NSA_GATE_COLS, Z_COLS, XBC_COLS, DT_COLS, MERGE_COLS)
IN_COLS = Q_COLS + KV_COLS + NSA_GATE_COLS + Z_COLS + XBC_COLS + DT_COLS + MERGE_COLS
EPS = 1e-6
NEG_INF = -1e30
FORCE = 1e4

kernel_name = "hybrid_nsa_mamba2_peer_block"


def split_columns(a, sizes):
    out, start = [], 0
    for s in sizes:
        out.append(a[..., start:start + s])
        start += s
    return out


def rmsnorm(x, g):
    xf = x.astype(jnp.float32)
    y = xf * lax.rsqrt(jnp.mean(xf * xf, axis=-1, keepdims=True) + EPS)
    return (y * g.astype(jnp.float32)).astype(x.dtype)


def partial_rope(x, pos):
    half = ROPE_DIM // 2
    inv = ROPE_THETA ** (-jnp.arange(half, dtype=jnp.float32) / half)
    ang = pos.astype(jnp.float32)[:, None] * inv[None, :]
    cos = jnp.cos(ang)[None, :, None, :].astype(x.dtype)
    sin = jnp.sin(ang)[None, :, None, :].astype(x.dtype)
    x1, x2, rest = x[..., :half], x[..., half:ROPE_DIM], x[..., ROPE_DIM:]
    return jnp.concatenate([x1 * cos - x2 * sin, x2 * cos + x1 * sin, rest], axis=-1)


def masked_softmax(s, mask):
    s = jnp.where(mask, s.astype(jnp.float32), NEG_INF)
    return jax.nn.softmax(s, axis=-1) * mask


def compress_tokens(kv, pos_emb, w1, b1, w2):
    B, S, G, hd = kv.shape
    ratio = CMP_BLOCK // CMP_STRIDE
    n_chunks = S // CMP_STRIDE
    nc = n_chunks - ratio + 1
    chunks = kv.reshape(B, n_chunks, CMP_STRIDE, G, hd)
    blocks = jnp.concatenate([chunks[:, r:r + nc] for r in range(ratio)], axis=2)
    blocks = blocks + pos_emb[None, None, :, None, :]
    flat = blocks.transpose(0, 1, 3, 2, 4).reshape(B, nc, G, CMP_BLOCK * hd)
    hidden = jax.nn.gelu(flat @ w1 + b1, approximate=False)
    return hidden @ w2


def selection_scores(p_cmp, n_sel_blocks):
    spc = SLC_BLOCK // CMP_STRIDE
    ratio = CMP_BLOCK // CMP_STRIDE
    n_off = spc + ratio - 1
    nc = p_cmp.shape[-1]
    need = spc * (n_sel_blocks - 1) + n_off
    front = ratio - 1
    p_pad = jnp.pad(p_cmp, ((0, 0), (0, 0), (0, 0), (front, need - front - nc)))
    weights = [float(sum(1 for r in range(ratio) if 0 <= o - front + r < spc)) for o in range(n_off)]
    terms = [p_pad[..., o:o + spc * (n_sel_blocks - 1) + 1:spc] for o in range(n_off)]
    return jnp.stack(terms, axis=-1) @ jnp.asarray(weights, p_cmp.dtype)


def native_sparse_attention(q, kv, gate_logits, cmp_pos, cmp_w1, cmp_b1, cmp_w2, pos):
    B, S, _ = q.shape
    G, R, hd = NSA_KV_HEADS, NSA_Q_PER_KV, NSA_HEAD_DIM
    scale = hd ** -0.5
    q = partial_rope(q.reshape(B, S, NSA_HEADS, hd), pos).reshape(B, S, G, R, hd)
    kv = kv.reshape(B, S, 3, 2, G, hd)
    gates = jax.nn.sigmoid(gate_logits.reshape(B, S, G, R, 3))
    kc = compress_tokens(partial_rope(kv[:, :, 0, 0], pos), cmp_pos[0], cmp_w1[0], cmp_b1[0], cmp_w2[0])
    vc = compress_tokens(kv[:, :, 0, 1], cmp_pos[1], cmp_w1[1], cmp_b1[1], cmp_w2[1])
    nc = kc.shape[1]
    c_end = jnp.arange(nc) * CMP_STRIDE + CMP_BLOCK - 1
    ns = S // SLC_BLOCK
    n_sel = min(SLC_TOP, ns)
    kb = partial_rope(kv[:, :, 1, 0], pos).reshape(B, ns, SLC_BLOCK, G, hd).transpose(0, 3, 1, 2, 4)
    vb = kv[:, :, 1, 1].reshape(B, ns, SLC_BLOCK, G, hd).transpose(0, 3, 1, 2, 4)
    b_ix = jnp.arange(B)[:, None, None, None]
    g_ix = jnp.arange(G)[None, :, None, None]
    blk = jnp.arange(ns)
    pad = ((0, 0), (WINDOW, 0), (0, 0), (0, 0))
    kw_pad = jnp.pad(partial_rope(kv[:, :, 2, 0], pos), pad)
    vw_pad = jnp.pad(kv[:, :, 2, 1], pad)

    def query_block(i):
        start = i * Q_BLOCK
        t = start + jnp.arange(Q_BLOCK)
        qb = lax.dynamic_slice_in_dim(q, start, Q_BLOCK, axis=1)
        gb = lax.dynamic_slice_in_dim(gates, start, Q_BLOCK, axis=1)
        s_c = jnp.einsum('btgrd,bcgd->bgrtc', qb, kc) * scale
        p_c = masked_softmax(s_c, c_end[None, :] <= t[:, None])
        o_c = jnp.einsum('bgrtc,bcgd->btgrd', p_c.astype(vc.dtype), vc)
        imp = selection_scores(p_c.sum(axis=2), ns)
        cur = t[:, None] // SLC_BLOCK
        forced = (blk[None, :] == 0) | (blk[None, :] == cur) | (blk[None, :] == cur - 1)
        imp = jnp.where(forced, FORCE, imp)
        imp = jnp.where(blk[None, :] * SLC_BLOCK <= t[:, None], imp, NEG_INF)
        _, top_i = lax.top_k(imp, n_sel)
        ks = kb[b_ix, g_ix, top_i]
        vs = vb[b_ix, g_ix, top_i]
        kpos = top_i[..., None] * SLC_BLOCK + jnp.arange(SLC_BLOCK)
        m_s = (kpos <= t[None, None, :, None, None]).reshape(B, G, 1, Q_BLOCK, n_sel * SLC_BLOCK)
        s_s = jnp.einsum('btgrd,bgtksd->bgrtks', qb, ks) * scale
        p_s = masked_softmax(s_s.reshape(B, G, R, Q_BLOCK, n_sel * SLC_BLOCK), m_s)
        o_s = jnp.einsum('bgrtn,bgtnd->btgrd', p_s.astype(vs.dtype), vs.reshape(B, G, Q_BLOCK, n_sel * SLC_BLOCK, hd))
        kw = lax.dynamic_slice_in_dim(kw_pad, start, WINDOW + Q_BLOCK, axis=1)
        vw = lax.dynamic_slice_in_dim(vw_pad, start, WINDOW + Q_BLOCK, axis=1)
        wpos = start - WINDOW + jnp.arange(WINDOW + Q_BLOCK)
        m_w = (wpos[None, :] <= t[:, None]) & (wpos[None, :] > t[:, None] - WINDOW) & (wpos[None, :] >= 0)
        s_w = jnp.einsum('btgrd,bsgd->bgrts', qb, kw) * scale
        p_w = masked_softmax(s_w, m_w)
        o_w = jnp.einsum('bgrts,bsgd->btgrd', p_w.astype(vw.dtype), vw)
        return gb[..., 0:1] * o_c + gb[..., 1:2] * o_s + gb[..., 2:3] * o_w

    out = lax.map(query_block, jnp.arange(S // Q_BLOCK))
    return out.transpose(1, 0, 2, 3, 4, 5).reshape(B, S, NSA_HEADS * hd)


def causal_depthwise_conv(u, w, b):
    out = lax.conv_general_dilated(u, w[:, None, :], window_strides=(1,), padding=((SSM_CONV - 1, 0),),
                                   dimension_numbers=('NWC', 'WIO', 'NWC'), feature_group_count=u.shape[-1])
    return out + b


def ssd_chunked(x, a, bm, cm):
    B, S, G, R, P = x.shape
    N = bm.shape[-1]
    L = SSM_CHUNK
    nc = S // L
    x = x.reshape(B, nc, L, G, R, P)
    a = a.reshape(B, nc, L, G, R).transpose(0, 3, 4, 1, 2)
    bm = bm.reshape(B, nc, L, G, N)
    cm = cm.reshape(B, nc, L, G, N)
    a_cs = jnp.cumsum(a, axis=-1)
    seg = a_cs[..., :, None] - a_cs[..., None, :]
    causal = jnp.tril(jnp.ones((L, L), dtype=bool))
    decay = jnp.where(causal, jnp.exp(jnp.where(causal, seg, 0.0)), 0.0)
    cb = jnp.einsum('bclgn,bcsgn->bgcls', cm, bm)
    y_diag = jnp.einsum('bgcls,bgrcls,bcsgrp->bclgrp', cb, decay, x)
    decay_to_end = jnp.exp(a_cs[..., -1:] - a_cs)
    states = jnp.einsum('bclgn,bgrcl,bclgrp->bcgrpn', bm, decay_to_end, x)
    chunk_decay = jnp.exp(a_cs[..., -1])

    def step(h, inp):
        st, dec = inp
        return h * dec[..., None, None] + st, h

    _, prev = lax.scan(step, jnp.zeros_like(states[:, 0]),
                       (states.transpose(1, 0, 2, 3, 4, 5), chunk_decay.transpose(3, 0, 1, 2)))
    prev = prev.transpose(1, 0, 2, 3, 4, 5)
    y_off = jnp.einsum('bclgn,bcgrpn,bgrcl->bclgrp', cm, prev, jnp.exp(a_cs))
    return (y_diag + y_off).reshape(B, S, G, R, P)


def mamba2_mixer(z, xbc, dt, conv_w, conv_b, dt_bias, a_log, d_skip, norm_w):
    B, S, _ = xbc.shape
    G, R, P, N = SSM_GROUPS, SSM_HEADS_PER_GROUP, SSM_HEAD_DIM, SSM_STATE
    xbc = jax.nn.silu(causal_depthwise_conv(xbc, conv_w, conv_b))
    xs, bm, cm = split_columns(xbc, (SSM_D_INNER, G * N, G * N))
    xs = xs.reshape(B, S, G, R, P)
    bm = bm.reshape(B, S, G, N)
    cm = cm.reshape(B, S, G, N)
    dt = jax.nn.softplus(dt.astype(jnp.float32) + dt_bias.astype(jnp.float32)).reshape(B, S, G, R)
    a = -jnp.exp(a_log.astype(jnp.float32)).reshape(G, R)
    y = ssd_chunked(xs * dt[..., None], dt * a, bm, cm)
    y = y + d_skip.reshape(G, R)[:, :, None] * xs
    g = (y.reshape(B, S, SSM_D_INNER) * jax.nn.silu(z.astype(jnp.float32))).reshape(B, S, G, -1)
    g = g * lax.rsqrt(jnp.mean(g * g, axis=-1, keepdims=True) + EPS)
    return (g.reshape(B, S, SSM_D_INNER) * norm_w).astype(z.dtype)


def peer_ffn(h, w_q, sub_keys, u, v):
    B, S, D = h.shape
    q = (h @ w_q).reshape(B, S, PEER_HEADS, 2, PEER_D_KEY // 2)
    s = jnp.einsum('bshcd,hckd->bshck', q, sub_keys).astype(jnp.float32)
    s_top, i_top = lax.top_k(s, PEER_TOPK)
    cand = s_top[..., 0, :, None] + s_top[..., 1, None, :]
    cand_idx = i_top[..., 0, :, None] * PEER_N_KEYS + i_top[..., 1, None, :]
    best_s, best_pos = lax.top_k(cand.reshape(B, S, PEER_HEADS, -1), PEER_TOPK)
    experts = jnp.take_along_axis(cand_idx.reshape(B, S, PEER_HEADS, -1), best_pos, axis=-1)
    gate = jax.nn.softmax(best_s, axis=-1)
    nb = S // TOKEN_BLOCK

    def to_blocks(a):
        return a.reshape(B, nb, TOKEN_BLOCK, *a.shape[2:]).swapaxes(0, 1)

    def token_block(args):
        hb, eb, gb = args
        ue = u[eb]
        act = jax.nn.gelu(jnp.einsum('btd,bthkd->bthk', hb, ue).astype(jnp.float32), approximate=False)
        return jnp.einsum('bthk,bthkd->btd', (gb * act).astype(h.dtype), v[eb])

    out = lax.map(token_block, (to_blocks(h), to_blocks(experts), to_blocks(gate)))
    return out.swapaxes(0, 1).reshape(B, S, D)


def setup_inputs(seed: int = 0) -> dict:
    key = jax.random.key(seed)
    ks = jax.random.split(key, 24)
    L, D, f32 = DEPTH, D_MODEL, jnp.float32

    def nrm(k, shape, scale):
        return scale * jax.random.normal(k, shape, f32)

    dt = jnp.exp(jax.random.uniform(ks[10], (L, SSM_HEADS), f32, math.log(1e-3), math.log(1e-1)))
    return {
        "x": nrm(ks[0], (BATCH, SEQ, D), 1.0),
        "attn_norm": 1.0 + nrm(ks[1], (L, D), 0.01),
        "w_in": nrm(ks[2], (L, D, IN_COLS), D ** -0.5),
        "gate_bias": nrm(ks[3], (L, 2, D), 0.01),
        "nsa_cmp_pos": nrm(ks[4], (L, 2, CMP_BLOCK, NSA_HEAD_DIM), 0.02),
        "nsa_cmp_w1": nrm(ks[5], (L, 2, CMP_BLOCK * NSA_HEAD_DIM, NSA_HEAD_DIM), (CMP_BLOCK * NSA_HEAD_DIM) ** -0.5),
        "nsa_cmp_b1": nrm(ks[6], (L, 2, NSA_HEAD_DIM), 0.01),
        "nsa_cmp_w2": nrm(ks[7], (L, 2, NSA_HEAD_DIM, NSA_HEAD_DIM), NSA_HEAD_DIM ** -0.5),
        "ssm_conv_w": nrm(ks[8], (L, SSM_CONV, SSM_CONV_CH), SSM_CONV ** -0.5),
        "ssm_conv_b": nrm(ks[9], (L, SSM_CONV_CH), 0.01),
        "ssm_dt_bias": dt + jnp.log(-jnp.expm1(-dt)),
        "ssm_a_log": jnp.log(jax.random.uniform(ks[11], (L, SSM_HEADS), f32, 1.0, 16.0)),
        "ssm_d": 1.0 + nrm(ks[12], (L, SSM_HEADS), 0.01),
        "ssm_norm": 1.0 + nrm(ks[13], (L, SSM_D_INNER), 0.01),
        "w_branch_nsa": nrm(ks[14], (L, NSA_HEADS * NSA_HEAD_DIM, D), (NSA_HEADS * NSA_HEAD_DIM) ** -0.5),
        "w_branch_ssm": nrm(ks[15], (L, SSM_D_INNER, D), SSM_D_INNER ** -0.5),
        "w_out": nrm(ks[16], (L, D, D), D ** -0.5),
        "ffn_norm": 1.0 + nrm(ks[17], (L, D), 0.01),
        "peer_w_q": nrm(ks[18], (L, D, PEER_HEADS * PEER_D_KEY), D ** -0.5),
        "peer_sub_keys": nrm(ks[19], (L, PEER_HEADS, 2, PEER_N_KEYS, PEER_D_KEY // 2), (PEER_D_KEY // 2) ** -0.5),
        "peer_u": nrm(ks[20], (L, PEER_N_EXPERTS, D), D ** -0.5),
        "peer_v": nrm(ks[21], (L, PEER_N_EXPERTS, D), 0.3),
        "final_norm": 1.0 + nrm(ks[22], (D,), 0.01),
    }


def reference(x, attn_norm, w_in, gate_bias, nsa_cmp_pos, nsa_cmp_w1, nsa_cmp_b1, nsa_cmp_w2,
              ssm_conv_w, ssm_conv_b, ssm_dt_bias, ssm_a_log, ssm_d, ssm_norm,
              w_branch_nsa, w_branch_ssm, w_out, ffn_norm, peer_w_q, peer_sub_keys, peer_u, peer_v,
              final_norm):
    B, S, D = x.shape
    pos = jnp.arange(S)
    for l in range(DEPTH):
        h = rmsnorm(x, attn_norm[l])
        proj = h @ w_in[l]
        q, kv, nsa_g, z, xbc, dt, merge_g = split_columns(proj, IN_SIZES)
        o_nsa = native_sparse_attention(q, kv, nsa_g, nsa_cmp_pos[l], nsa_cmp_w1[l], nsa_cmp_b1[l],
                                        nsa_cmp_w2[l], pos)
        o_ssm = mamba2_mixer(z, xbc, dt, ssm_conv_w[l], ssm_conv_b[l], ssm_dt_bias[l], ssm_a_log[l],
                             ssm_d[l], ssm_norm[l])
        g = jax.nn.sigmoid(merge_g.reshape(B, S, 2, D) + gate_bias[l])
        merged = g[:, :, 0] * (o_nsa @ w_branch_nsa[l]) + g[:, :, 1] * (o_ssm @ w_branch_ssm[l])
        x = x + merged @ w_out[l]
        x = x + peer_ffn(rmsnorm(x, ffn_norm[l]), peer_w_q[l], peer_sub_keys[l], peer_u[l], peer_v[l])
    return rmsnorm(x, final_norm)
```

```python
import functools

import numpy as np
import jax
import jax.numpy as jnp
from jax import lax
from jax.experimental import pallas as pl
from jax.experimental.pallas import tpu as pltpu

F32, BF16, I32 = jnp.float32, jnp.bfloat16, jnp.int32

D_MODEL = 2048
NSA_HEADS, NSA_GROUPS, NSA_R, HD = 16, 4, 4, 128
CMP_BLOCK, CMP_STRIDE = 32, 16
SLC_BLOCK, SLC_TOP = 64, 16
WINDOW, QB = 512, 128
ROPE_THETA, ROPE_DIM = 500000.0, 32
SSM_HEADS, SSM_P, SSM_G, SSM_N, SSM_CONV, SSM_CHUNK = 32, 64, 4, 128, 4, 128
SSM_INNER = SSM_HEADS * SSM_P
PEER_HEADS, PEER_KEYS, PEER_TOPK = 8, 128, 16
PEER_EXPERTS = PEER_KEYS * PEER_KEYS
EPS, NEG, FORCE = 1e-6, -1e30, 1e4

Z0, XS0, MG0, BC0, SM0, QKV0, NPROJ = 0, 2048, 4096, 8192, 9216, 9728, 14848
QKV_COLS = NPROJ - QKV0
GATE_COLS, DT_COLS = 3 * NSA_HEADS, SSM_HEADS

VMEM_LIMIT = 56 * 1024 * 1024


def _params(sem):
    return pltpu.CompilerParams(dimension_semantics=sem, vmem_limit_bytes=VMEM_LIMIT)


def _dot(a, b):
    return jnp.dot(a, b, preferred_element_type=F32)


def _dot_nt(a, b):
    return lax.dot_general(a, b, (((1,), (1,)), ((), ())), preferred_element_type=F32)


def _split3(x):
    hi = x.astype(BF16)
    r1 = x - hi.astype(F32)
    mid = r1.astype(BF16)
    lo = (r1 - mid.astype(F32)).astype(BF16)
    return hi, mid, lo


def _dot_f32_left(x, b):
    hi, mid, lo = _split3(x)
    return _dot(hi, b) + _dot(mid, b) + _dot(lo, b)


def _dot_f32_right(a, x):
    hi, mid, lo = _split3(x)
    return _dot(a, hi) + _dot(a, mid) + _dot(a, lo)


def _gelu(x):
    return 0.5 * x * (1.0 + lax.erf(x * np.float32(1.0 / np.sqrt(2.0))))


def _silu(x):
    return x * jax.nn.sigmoid(x)


def _softplus(x):
    return jnp.maximum(x, 0.0) + jnp.log1p(jnp.exp(-jnp.abs(x)))


def _inproj_kernel(x_ref, g_ref, w_ref, wqkv_ref, cos_ref, sin_ref, o_ref, qkv_ref, h_scr, acc_scr, *, n_plain):
    j = pl.program_id(1)

    @pl.when(j == 0)
    def _():
        x = x_ref[...]
        y = x * lax.rsqrt(jnp.mean(x * x, axis=-1, keepdims=True) + EPS)
        h_scr[...] = (y * g_ref[...]).astype(BF16)

    @pl.when(j < n_plain)
    def _():
        o_ref[...] = _dot(h_scr[...], w_ref[...])

    @pl.when(j >= n_plain)
    def _():
        acc_scr[...] = _dot(h_scr[...], wqkv_ref[...])
        jj = j - n_plain
        is_rope = (jj < 5) | (jj == 6) | (jj == 8)

        @pl.when(is_rope)
        def _():
            scale = jnp.where(jj < 4, np.float32(HD ** -0.5), np.float32(1.0))
            cos, sin = cos_ref[...], sin_ref[...]
            lane = lax.broadcasted_iota(I32, cos.shape, 1)
            for t in range(4):
                x = acc_scr[:, t * HD:(t + 1) * HD]
                rot = jnp.where(lane < ROPE_DIM // 2, pltpu.roll(x, HD - ROPE_DIM // 2, axis=1),
                                pltpu.roll(x, ROPE_DIM // 2, axis=1))
                qkv_ref[:, t * HD:(t + 1) * HD] = ((x * cos + rot * sin) * scale).astype(BF16)

        @pl.when(jnp.logical_not(is_rope))
        def _():
            qkv_ref[...] = acc_scr[...].astype(BF16)


def _stage_inproj(x2, attn_norm, w_cat, w_all):
    s, d = x2.shape
    tm, tn = min(1024, s), 512
    n_plain = QKV0 // tn
    cos_full, sin_signed = _rope_tables(s)
    return pl.pallas_call(
        functools.partial(_inproj_kernel, n_plain=n_plain),
        grid=(s // tm, NPROJ // tn),
        in_specs=[pl.BlockSpec((tm, d), lambda i, j: (i, 0)),
                  pl.BlockSpec((1, d), lambda i, j: (0, 0)),
                  pl.BlockSpec((d, tn), lambda i, j: (0, jnp.minimum(j, n_plain - 1))),
                  pl.BlockSpec((d, tn), lambda i, j: (0, jnp.maximum(j - n_plain, 0))),
                  pl.BlockSpec((tm, HD), lambda i, j: (i, 0)),
                  pl.BlockSpec((tm, HD), lambda i, j: (i, 0))],
        out_specs=[pl.BlockSpec((tm, tn), lambda i, j: (i, jnp.minimum(j, n_plain - 1))),
                   pl.BlockSpec((tm, tn), lambda i, j: (i, jnp.maximum(j - n_plain, 0)))],
        out_shape=[jax.ShapeDtypeStruct((s, QKV0), F32), jax.ShapeDtypeStruct((s, QKV_COLS), BF16)],
        scratch_shapes=[pltpu.VMEM((tm, d), BF16), pltpu.VMEM((tm, tn), F32)],
        compiler_params=_params(("parallel", "arbitrary")),
        name="inproj",
    )(x2, attn_norm.reshape(1, d), w_cat, w_all, cos_full, sin_signed)


def _rope_tables(s):
    half = ROPE_DIM // 2
    inv = ROPE_THETA ** (-jnp.arange(half, dtype=F32) / half)
    ang = jnp.arange(s).astype(F32)[:, None] * inv[None, :]
    cos, sin = jnp.cos(ang), jnp.sin(ang)
    rest = HD - ROPE_DIM
    cos_full = jnp.concatenate([cos, cos, jnp.ones((s, rest), F32)], axis=1)
    sin_signed = jnp.concatenate([-sin, sin, jnp.zeros((s, rest), F32)], axis=1)
    return cos_full, sin_signed


def _compress_kernel(x_ref, w1_ref, pos_ref, b1_ref, w2_ref, o_ref, x_scr):
    x_scr[...] = x_ref[...].astype(F32)
    w1 = w1_ref[0]
    nc = o_ref.shape[1]
    a = b = None
    for j in range(CMP_STRIDE):
        xj = x_scr[pl.ds(j, nc, stride=CMP_STRIDE), :].astype(BF16)
        da = _dot(xj, w1[j * HD:(j + 1) * HD])
        db = _dot(xj, w1[(CMP_STRIDE + j) * HD:(CMP_STRIDE + j + 1) * HD])
        a, b = (da, db) if a is None else (a + da, b + db)
    b_next = pltpu.roll(b, nc - 1, axis=0)
    pos_term = _dot(pos_ref[0], w1)[0:1]
    hid = _gelu(a + b_next + pos_term + b1_ref[0])
    o_ref[0] = _dot(hid.astype(BF16), w2_ref[0]).astype(BF16)


def _stage_compress(qkv, cmp_pos, cmp_w1, cmp_b1, cmp_w2):
    s = qkv.shape[0]
    nc = s // CMP_STRIDE
    pos = jnp.broadcast_to(cmp_pos.reshape(2, 1, CMP_BLOCK * HD), (2, 8, CMP_BLOCK * HD)).astype(BF16)
    return pl.pallas_call(
        _compress_kernel,
        grid=(8,),
        in_specs=[pl.BlockSpec((s, HD), lambda n: (0, 16 + n)),
                  pl.BlockSpec((1, CMP_BLOCK * HD, HD), lambda n: (n // 4, 0, 0)),
                  pl.BlockSpec((1, 8, CMP_BLOCK * HD), lambda n: (n // 4, 0, 0)),
                  pl.BlockSpec((1, 1, HD), lambda n: (n // 4, 0, 0)),
                  pl.BlockSpec((1, HD, HD), lambda n: (n // 4, 0, 0))],
        out_specs=pl.BlockSpec((1, nc, HD), lambda n: (n, 0, 0)),
        out_shape=jax.ShapeDtypeStruct((8, nc, HD), BF16),
        scratch_shapes=[pltpu.VMEM((s, HD), F32)],
        compiler_params=_params(("parallel",)),
        name="compress",
    )(qkv, cmp_w1.astype(BF16), pos, cmp_b1.reshape(2, 1, HD), cmp_w2.astype(BF16))


def _selection_weights(nc):
    w = np.zeros((nc, 128), np.float32)
    spc = SLC_BLOCK // CMP_STRIDE
    for c in range(nc - 1):
        for j in range(128):
            o = c - spc * j + 1
            if 0 <= o <= 4:
                w[c, j] = 1.0 if o in (0, 4) else 2.0
    return w


def _block_expand(s):
    e = np.zeros((s, 128), np.float32)
    for j in range(min(128, s // SLC_BLOCK)):
        e[j * SLC_BLOCK:(j + 1) * SLC_BLOCK, j] = 1.0
    return e


def _rank_select_bias(imp_scr, n_v):
    tiles = [imp_scr[8 * v:8 * v + 8, :] for v in range(n_v)]
    ranks = [jnp.zeros((8, QB), I32) for _ in range(n_v)]
    sub = lax.broadcasted_iota(I32, (8, QB), 0)
    for iv in range(n_v):
        for ii in range(8):
            row = jnp.broadcast_to(imp_scr[iv * 8 + ii:iv * 8 + ii + 1, :], (8, QB))
            for jv in range(n_v):
                if jv > iv:
                    beats = row >= tiles[jv]
                elif jv < iv:
                    beats = row > tiles[jv]
                else:
                    beats = (row > tiles[jv]) | ((row == tiles[jv]) & (sub > ii))
                ranks[jv] = ranks[jv] + beats.astype(I32)
    parts = [jnp.where(r < SLC_TOP, 0.0, NEG) for r in ranks]
    if n_v < 16:
        parts.append(jnp.full(((16 - n_v) * 8, QB), NEG, F32))
    return jnp.concatenate(parts, axis=0)


def _attn_kernel(q_ref, kc_ref, vc_ref, ks_ref, vs_ref, kw_ref, vw_ref, gl_ref, esel_ref, wsel_ref,
                 o_ref, imp_scr, bias_scr, s_scr, *, tk):
    i = pl.program_id(1)
    start = i * QB
    rows = NSA_R * QB
    qb = q_ref[...]
    q_rows = jnp.concatenate([qb[:, r * HD:(r + 1) * HD] for r in range(NSA_R)], axis=0)
    t_rows = start + (lax.broadcasted_iota(I32, (rows, 1), 0) & (QB - 1))

    kc, vc = kc_ref[0], vc_ref[0]
    ncp = kc.shape[0]
    s_c = _dot_nt(q_rows, kc)
    c_end = lax.broadcasted_iota(I32, (1, ncp), 1) * CMP_STRIDE + (CMP_BLOCK - 1)
    s_c = jnp.where(c_end <= t_rows, s_c, NEG)
    p_c = jnp.exp(s_c - jnp.max(s_c, axis=-1, keepdims=True))
    inv_l = jnp.where(t_rows >= CMP_BLOCK - 1, 1.0 / jnp.sum(p_c, axis=-1, keepdims=True), 0.0)
    p_c = p_c * inv_l
    o_c = _dot(p_c.astype(BF16), vc)

    p_sum = p_c[0:QB]
    for r in range(1, NSA_R):
        p_sum = p_sum + p_c[r * QB:(r + 1) * QB]
    imp = _dot_f32_left(p_sum, wsel_ref[...])
    t_q = start + lax.broadcasted_iota(I32, (QB, 1), 0)
    blk = lax.broadcasted_iota(I32, (1, 128), 1)
    cur = t_q >> 6
    forced = (blk == 0) | (blk == cur) | (blk == cur - 1)
    imp = jnp.where(forced, FORCE, imp)
    imp = jnp.where(blk * SLC_BLOCK <= t_q, imp, NEG)
    imp_scr[...] = imp.T
    n_blocks = 2 * i + 2
    for q in range(4):
        n_v = 4 * (q + 1)

        @pl.when((n_blocks > 8 * (n_v - 4)) & ((n_blocks <= 8 * n_v) | (q == 3)))
        def _(n_v=n_v):
            bias_scr[...] = _rank_select_bias(imp_scr, n_v)

    bias = bias_scr[...].T.astype(BF16)
    q_aug = jnp.concatenate([q_rows, jnp.concatenate([bias] * NSA_R, axis=0)], axis=1)

    def scores(kt):
        k0 = pl.multiple_of(kt * tk, tk)
        k_aug = jnp.concatenate([ks_ref[pl.ds(k0, tk), :], esel_ref[pl.ds(k0, tk), :]], axis=1)
        return _dot_nt(q_aug, k_aug)

    def online_softmax(s, carry):
        m, l, acc = carry
        m_new = jnp.maximum(m, jnp.max(s, axis=-1, keepdims=True))
        alpha = jnp.exp(m - m_new)
        p = jnp.exp(s - m_new)
        return m_new, alpha * l + jnp.sum(p, axis=-1, keepdims=True), alpha, acc, p.astype(BF16)

    def body(kt, carry):
        m, l, alpha, acc, p = online_softmax(s_scr[...], carry)
        s_scr[...] = scores(kt + 1)
        k0 = pl.multiple_of(kt * tk, tk)
        return m, l, alpha * acc + _dot(p, vs_ref[pl.ds(k0, tk), :])

    n_full = start // tk
    s_scr[...] = scores(0)
    init = (jnp.full((rows, 1), NEG, F32), jnp.zeros((rows, 1), F32), jnp.zeros((rows, HD), F32))
    carry = lax.fori_loop(0, n_full, body, init)
    k_last = pl.multiple_of(n_full * tk, tk)
    kpos = k_last + lax.broadcasted_iota(I32, (1, tk), 1)
    _, l_s, alpha, acc, p = online_softmax(jnp.where(kpos <= t_rows, s_scr[...], NEG), carry)
    o_s = (alpha * acc + _dot(p, vs_ref[pl.ds(k_last, tk), :])) / l_s

    wlen = WINDOW + QB
    w0 = pl.multiple_of(jnp.maximum(start - WINDOW, 0), QB)
    kw = kw_ref[pl.ds(w0, wlen), :]
    vw = vw_ref[pl.ds(w0, wlen), :]
    s_w = _dot_nt(q_rows, kw)
    wpos = w0 + lax.broadcasted_iota(I32, (1, wlen), 1)
    back = lax.bitcast_convert_type(t_rows - wpos, jnp.uint32)
    s_w = jnp.where(back < WINDOW, s_w, NEG)
    p_w = jnp.exp(s_w - jnp.max(s_w, axis=-1, keepdims=True))
    o_w = _dot(p_w.astype(BF16), vw) / jnp.sum(p_w, axis=-1, keepdims=True)

    gates = jax.nn.sigmoid(gl_ref[:, 0:GATE_COLS])
    for gg in range(NSA_GROUPS):

        @pl.when(pl.program_id(0) == gg)
        def _(gg=gg):
            for r in range(NSA_R):
                sl = slice(r * QB, (r + 1) * QB)
                c = (gg * NSA_R + r) * 3
                o = gates[:, c:c + 1] * o_c[sl] + gates[:, c + 1:c + 2] * o_s[sl] + gates[:, c + 2:c + 3] * o_w[sl]
                o_ref[:, r * HD:(r + 1) * HD] = o.astype(BF16)


def _stage_attention(qkv, kvc, proj):
    s = qkv.shape[0]
    nc = s // CMP_STRIDE
    tk = 1024
    esel = jnp.asarray(_block_expand(s), BF16)
    wsel = jnp.asarray(_selection_weights(nc), BF16)
    col = lambda c: (lambda g, i: (0, c + g))
    return pl.pallas_call(
        functools.partial(_attn_kernel, tk=tk),
        grid=(NSA_GROUPS, s // QB),
        in_specs=[pl.BlockSpec((QB, NSA_R * HD), lambda g, i: (i, g)),
                  pl.BlockSpec((1, nc, HD), lambda g, i: (g, 0, 0)),
                  pl.BlockSpec((1, nc, HD), lambda g, i: (NSA_GROUPS + g, 0, 0)),
                  pl.BlockSpec((s, HD), col(24)),
                  pl.BlockSpec((s, HD), col(28)),
                  pl.BlockSpec((s, HD), col(32)),
                  pl.BlockSpec((s, HD), col(36)),
                  pl.BlockSpec((QB, 128), lambda g, i: (i, SM0 // 128)),
                  pl.BlockSpec((s, 128), lambda g, i: (0, 0)),
                  pl.BlockSpec((nc, 128), lambda g, i: (0, 0))],
        out_specs=pl.BlockSpec((QB, NSA_R * HD), lambda g, i: (i, g)),
        out_shape=jax.ShapeDtypeStruct((s, NSA_HEADS * HD), BF16),
        scratch_shapes=[pltpu.VMEM((128, QB), F32), pltpu.VMEM((128, QB), F32),
                        pltpu.VMEM((NSA_R * QB, tk), F32)],
        compiler_params=_params(("parallel", "arbitrary")),
        name="attention",
    )(qkv, kvc, kvc, qkv, qkv, qkv, qkv, proj, esel, wsel)


def _conv_silu(cur_ref, prev_ref, w_ref, b_ref, first):
    cur = cur_ref[...]
    prev = jnp.where(first, 0.0, prev_ref[...])
    full = jnp.concatenate([prev, cur], axis=0)
    n = cur.shape[0]
    acc = b_ref[...] + full[8 - (SSM_CONV - 1):8 - (SSM_CONV - 1) + n] * w_ref[0:1, :]
    for w in range(1, SSM_CONV):
        off = 8 - (SSM_CONV - 1) + w
        acc = acc + full[off:off + n] * w_ref[w:w + 1, :]
    return _silu(acc)


def _mamba_kernel(z_ref, xs_ref, xsp_ref, bc_ref, bcp_ref, sm_ref, dtt_ref,
                  cwx_ref, cbx_ref, cwb_ref, cbb_ref, dtb_ref, dtbt_ref, a_ref, at_ref,
                  dx_ref, nw_ref, ex_ref, o_ref, st_scr):
    c = pl.program_id(0)
    first = c == 0
    L, N, GW = SSM_CHUNK, SSM_N, SSM_INNER // SSM_G

    @pl.when(first)
    def _():
        st_scr[...] = jnp.zeros_like(st_scr)

    xs = _conv_silu(xs_ref, xsp_ref, cwx_ref, cbx_ref, first)
    bc = _conv_silu(bc_ref, bcp_ref, cwb_ref, cbb_ref, first)
    ex = ex_ref[...]

    dt = _softplus(sm_ref[:, GATE_COLS:GATE_COLS + DT_COLS] + dtb_ref[...])
    a = dt * a_ref[...]
    li = lax.broadcasted_iota(I32, (L, L), 0)
    si = lax.broadcasted_iota(I32, (L, L), 1)
    causal = li >= si
    a_cs = _dot_f32_right(causal.astype(BF16), a)
    dt_t = _softplus(dtt_ref[...] + dtbt_ref[...])
    a_t = dt_t * at_ref[...]
    a_cs_t = _dot_f32_left(a_t, (li <= si).astype(BF16))

    a_last = a_cs[L - 1:L, :]
    xdt = xs * _dot_f32_left(dt, ex)
    ea_x = _dot_f32_left(jnp.exp(a_cs), ex)
    xdte = (xdt * _dot_f32_left(jnp.exp(a_last - a_cs), ex)).astype(BF16)
    cd_x = _dot_f32_left(jnp.broadcast_to(jnp.exp(a_last), (8, SSM_HEADS)), ex)[0:1]
    xdt_b = xdt.astype(BF16)
    lane = lax.broadcasted_iota(I32, (L, 2 * SSM_P), 1)

    y_parts = []
    for g in range(SSM_G):
        bm = bc[:, g * N:(g + 1) * N]
        cm = bc[:, SSM_G * N + g * N:SSM_G * N + (g + 1) * N].astype(BF16)
        bm_b = bm.astype(BF16)
        cb = _dot_nt(cm, bm_b)
        gs = slice(g * GW, (g + 1) * GW)
        st = st_scr[g]
        y_off = _dot(cm, st.astype(BF16)) * ea_x[:, gs]
        st_scr[g] = st * cd_x[:, gs] + _dot(bm.T.astype(BF16), xdte[:, gs])
        for pair in range(SSM_HEADS // SSM_G // 2):
            h0 = g * (SSM_HEADS // SSM_G) + 2 * pair
            ys = []
            for hh in (h0, h0 + 1):
                seg = a_cs[:, hh:hh + 1] - a_cs_t[hh:hh + 1, :]
                dec = jnp.where(causal, jnp.exp(jnp.where(causal, seg, 0.0)), 0.0)
                ys.append(_dot((cb * dec).astype(BF16), xdt_b[:, h0 * SSM_P:(h0 + 2) * SSM_P]))
            y_pair = jnp.where(lane < SSM_P, ys[0], ys[1])
            off = (h0 - g * (SSM_HEADS // SSM_G)) * SSM_P
            y_parts.append(y_pair + y_off[:, off:off + 2 * SSM_P])
    y = jnp.concatenate(y_parts, axis=1) + dx_ref[...] * xs

    gz = y * _silu(z_ref[...])
    for g in range(SSM_G):
        gs = slice(g * GW, (g + 1) * GW)
        blk = gz[:, gs]
        ms = jnp.mean(blk * blk, axis=-1, keepdims=True)
        o_ref[:, gs] = (blk * lax.rsqrt(ms + EPS) * nw_ref[:, gs]).astype(BF16)


def _stage_mamba(proj, conv_w, conv_b, dt_bias, a_log, d_skip, norm_w):
    s = proj.shape[0]
    L = SSM_CHUNK
    dt_t = proj[:, SM0 + GATE_COLS:SM0 + GATE_COLS + DT_COLS].T
    a = -jnp.exp(a_log.astype(F32))
    ex = jnp.asarray(np.kron(np.eye(SSM_HEADS, dtype=np.float32), np.ones((1, SSM_P), np.float32)), BF16)
    d_x = jnp.repeat(d_skip.astype(F32), SSM_P).reshape(1, SSM_INNER)
    prev = lambda blk: (lambda c: (jnp.maximum(c * (L // 8) - 1, 0), blk))
    full = lambda shape: pl.BlockSpec(shape, lambda c: (0, 0))
    return pl.pallas_call(
        _mamba_kernel,
        grid=(s // L,),
        in_specs=[pl.BlockSpec((L, SSM_INNER), lambda c: (c, Z0 // SSM_INNER)),
                  pl.BlockSpec((L, SSM_INNER), lambda c: (c, XS0 // SSM_INNER)),
                  pl.BlockSpec((8, SSM_INNER), prev(XS0 // SSM_INNER)),
                  pl.BlockSpec((L, 1024), lambda c: (c, BC0 // 1024)),
                  pl.BlockSpec((8, 1024), prev(BC0 // 1024)),
                  pl.BlockSpec((L, 512), lambda c: (c, SM0 // 512)),
                  pl.BlockSpec((SSM_HEADS, L), lambda c: (0, c)),
                  full((SSM_CONV, SSM_INNER)), full((1, SSM_INNER)),
                  full((SSM_CONV, 1024)), full((1, 1024)),
                  full((1, SSM_HEADS)), full((SSM_HEADS, 1)),
                  full((1, SSM_HEADS)), full((SSM_HEADS, 1)),
                  full((1, SSM_INNER)), full((1, SSM_INNER)), full((SSM_HEADS, SSM_INNER))],
        out_specs=pl.BlockSpec((L, SSM_INNER), lambda c: (c, 0)),
        out_shape=jax.ShapeDtypeStruct((s, SSM_INNER), BF16),
        scratch_shapes=[pltpu.VMEM((SSM_G, SSM_N, SSM_INNER // SSM_G), F32)],
        compiler_params=_params(("arbitrary",)),
        name="mamba",
    )(proj, proj, proj, proj, proj, proj, dt_t,
      conv_w[:, :SSM_INNER], conv_b[:SSM_INNER].reshape(1, -1),
      conv_w[:, SSM_INNER:], conv_b[SSM_INNER:].reshape(1, -1),
      dt_bias.reshape(1, -1), dt_bias.reshape(-1, 1), a.reshape(1, -1), a.reshape(-1, 1),
      d_x, norm_w.reshape(1, -1), ex)


def _merge_kernel(on_ref, os_ref, mg_ref, gb_ref, x_ref, wn_ref, ws_ref, wo_ref, fn_ref, x1_ref, h2_ref):
    d = D_MODEL
    g0 = jax.nn.sigmoid(mg_ref[:, :d] + gb_ref[0:1, :])
    g1 = jax.nn.sigmoid(mg_ref[:, d:] + gb_ref[1:2, :])
    merged = g0 * _dot(on_ref[...], wn_ref[...]) + g1 * _dot(os_ref[...], ws_ref[...])
    x1 = x_ref[...] + _dot(merged.astype(BF16), wo_ref[...])
    x1_ref[...] = x1
    y = x1 * lax.rsqrt(jnp.mean(x1 * x1, axis=-1, keepdims=True) + EPS)
    h2_ref[...] = (y * fn_ref[...]).astype(BF16)


def _stage_merge(o_nsa, o_ssm, proj, gate_bias, x2, w_bn, w_bs, w_out, ffn_norm):
    s, d = x2.shape
    tm = min(256, s)
    row = lambda w: pl.BlockSpec((tm, w), lambda i: (i, 0))
    res = lambda shape: pl.BlockSpec(shape, lambda i: (0, 0), pipeline_mode=pl.Buffered(1))
    return pl.pallas_call(
        _merge_kernel,
        grid=(s // tm,),
        in_specs=[row(d), row(d),
                  pl.BlockSpec((tm, 2 * d), lambda i: (i, MG0 // (2 * d))),
                  res((2, d)), row(d), res((d, d)), res((d, d)), res((d, d)), res((1, d))],
        out_specs=[row(d), row(d)],
        out_shape=[jax.ShapeDtypeStruct((s, d), F32), jax.ShapeDtypeStruct((s, d), BF16)],
        compiler_params=_params(("parallel",)),
        name="merge",
    )(o_nsa, o_ssm, proj, gate_bias, x2, w_bn.astype(BF16), w_bs.astype(BF16), w_out.astype(BF16),
      ffn_norm.reshape(1, d))


def _top_rows_ranked(x, k, with_rank=True):
    n = x.shape[0]
    io = lax.broadcasted_iota(I32, x.shape, 0)
    rank = jnp.full(x.shape, float(n), F32)
    rows = []
    for r in range(k):
        m = jnp.max(x, axis=0, keepdims=True)
        first = jnp.min(jnp.where(x == m, io, n), axis=0, keepdims=True)
        hit = io == first
        x = jnp.where(hit, -jnp.inf, x)
        if with_rank:
            rank = jnp.where(hit, float(r), rank)
        rows.append(m)
    return jnp.concatenate(rows, axis=0), (rank if with_rank else x)


_CAND_COUNTS = [PEER_TOPK // (i + 1) for i in range(PEER_TOPK)]


def _peer_score_kernel(wq_ref, h_ref, keys_ref, rk_ref, nb_ref, e1_ref, e2_ref):
    q_t = _dot_nt(wq_ref[...], h_ref[...]).astype(BF16)
    kd = PEER_KEYS
    t = q_t.shape[1]
    sub8 = lax.broadcasted_iota(I32, (8, t), 0)
    for h in range(PEER_HEADS):
        s1 = _dot(keys_ref[2 * h], q_t[(2 * h) * kd:(2 * h + 1) * kd])
        s2 = _dot(keys_ref[2 * h + 1], q_t[(2 * h + 1) * kd:(2 * h + 2) * kd])
        top1, rank1 = _top_rows_ranked(s1, PEER_TOPK)
        top2, rank2 = _top_rows_ranked(s2, PEER_TOPK)
        groups = [top1[0:1] + top2, top1[1:2] + top2[0:8]]
        for i in range(2, 8):
            groups.append(jnp.where(sub8 < _CAND_COUNTS[i], top1[i:i + 1] + top2[0:8], -jnp.inf))
        groups.append(top1[8:16] + top2[0:1])
        cand = jnp.concatenate(groups, axis=0)
        best, left = _top_rows_ranked(cand, PEER_TOPK, with_rank=False)
        picked = ((left == -jnp.inf) & (cand > -jnp.inf)).astype(F32)
        nb = [jnp.sum(picked[0:16], axis=0, keepdims=True)]
        nb += [jnp.sum(picked[8 + 8 * i:16 + 8 * i], axis=0, keepdims=True) for i in range(1, 8)]
        nb += [picked[72 + r:73 + r] for r in range(8)]
        nb_a = jnp.zeros_like(s1)
        for i in range(PEER_TOPK):
            nb_a = jnp.where(rank1 == float(i), nb[i], nb_a)
        inv_z = 1.0 / jnp.sum(jnp.exp(best - best[0:1]), axis=0, keepdims=True)
        rk_ref[h] = rank2.astype(BF16)
        nb_ref[h] = nb_a
        e1_ref[h] = jnp.exp(s1 - top1[0:1]) * inv_z
        e2_ref[h] = jnp.exp(s2 - top2[0:1]).astype(BF16)


def _stage_peer_scores(h2, w_q, sub_keys):
    s, d = h2.shape
    t = min(256, s)
    big = jax.ShapeDtypeStruct((PEER_HEADS, PEER_KEYS, s), F32)
    big_b = jax.ShapeDtypeStruct((PEER_HEADS, PEER_KEYS, s), BF16)
    big_spec = pl.BlockSpec((PEER_HEADS, PEER_KEYS, t), lambda i: (0, 0, i))
    return pl.pallas_call(
        _peer_score_kernel,
        grid=(s // t,),
        in_specs=[pl.BlockSpec((d, d), lambda i: (0, 0), pipeline_mode=pl.Buffered(1)),
                  pl.BlockSpec((t, d), lambda i: (i, 0)),
                  pl.BlockSpec((2 * PEER_HEADS, PEER_KEYS, PEER_KEYS), lambda i: (0, 0, 0))],
        out_specs=[big_spec, big_spec, big_spec, big_spec],
        out_shape=[big_b, big, big, big_b],
        compiler_params=_params(("parallel",)),
        name="peer_scores",
    )(w_q.T.astype(BF16), h2, sub_keys.reshape(2 * PEER_HEADS, PEER_KEYS, PEER_KEYS).astype(BF16))


def _peer_dense_kernel(u_ref, v_ref, h_ref, rk_ref, nb_ref, e1_ref, e2_ref, x1_ref, g_ref, o_ref, acc_scr, w_scr,
                       *, n_a, normalize):
    j = pl.program_id(1)
    kd = PEER_KEYS
    t = h_ref.shape[0]

    @pl.when(j == 0)
    def _():
        acc_scr[...] = jnp.zeros_like(acc_scr)

    act = _dot_nt(u_ref[...], h_ref[...])
    for aa in range(n_a):
        rows = slice(aa * kd, (aa + 1) * kd)
        def packed_row(ref, h):
            return jnp.broadcast_to(ref[h, aa:aa + 1, :], (16, t)).astype(BF16)

        nb_rows = [packed_row(nb_ref, h) for h in range(PEER_HEADS)]
        e1_rows = [packed_row(e1_ref, h) for h in range(PEER_HEADS)]
        zero = jnp.zeros((kd, 128), BF16)
        for lt in range(t // 128):
            ls = slice(lt * 128, (lt + 1) * 128)
            wsum = None
            for h in range(PEER_HEADS):
                nb_t = jnp.concatenate([nb_rows[h][:, ls]] * (kd // 16), axis=0)
                e1_t = jnp.concatenate([e1_rows[h][:, ls]] * (kd // 16), axis=0)
                term = jnp.where(rk_ref[h, :, ls] < nb_t, e2_ref[h, :, ls], zero) * e1_t
                wsum = term if wsum is None else wsum + term
            w_scr[rows, ls] = _gelu(act[rows, ls]).astype(BF16) * wsum
    acc_scr[...] += lax.dot_general(v_ref[...], w_scr[...], (((0,), (0,)), ((), ())), preferred_element_type=F32)

    @pl.when(j == pl.num_programs(1) - 1)
    def _():
        x = x1_ref[...] + acc_scr[...].T
        if normalize:
            x = x * lax.rsqrt(jnp.mean(x * x, axis=-1, keepdims=True) + EPS) * g_ref[...]
        o_ref[...] = x


def _stage_peer_dense(h2, u_b, v_b, rk, nb, e1, e2, x1, norm_w):
    s, d = h2.shape
    t = min(512, s)
    n_a = 8
    ec = n_a * PEER_KEYS
    normalize = norm_w is not None
    g = norm_w.reshape(1, d) if normalize else jnp.ones((1, d), F32)
    big_spec = pl.BlockSpec((PEER_HEADS, PEER_KEYS, t), lambda i, j: (0, 0, i))
    row_spec = pl.BlockSpec((PEER_HEADS, n_a, t), lambda i, j: (0, j, i))
    return pl.pallas_call(
        functools.partial(_peer_dense_kernel, n_a=n_a, normalize=normalize),
        grid=(s // t, PEER_EXPERTS // ec),
        in_specs=[pl.BlockSpec((ec, d), lambda i, j: (j, 0)),
                  pl.BlockSpec((ec, d), lambda i, j: (j, 0)),
                  pl.BlockSpec((t, d), lambda i, j: (i, 0)),
                  big_spec, row_spec, row_spec, big_spec,
                  pl.BlockSpec((t, d), lambda i, j: (i, 0)),
                  pl.BlockSpec((1, d), lambda i, j: (0, 0))],
        out_specs=pl.BlockSpec((t, d), lambda i, j: (i, 0)),
        out_shape=jax.ShapeDtypeStruct((s, d), F32),
        scratch_shapes=[pltpu.VMEM((d, t), F32), pltpu.VMEM((ec, t), BF16)],
        compiler_params=_params(("parallel", "arbitrary")),
        name="peer_dense",
    )(u_b, v_b, h2, rk, nb, e1, e2, x1, g)


def _cat_in_weights(w):
    w = w.astype(BF16)
    gates, z, xs, bc = w[:, 5120:5168], w[:, 5168:7216], w[:, 7216:9264], w[:, 9264:10288]
    dt, mg = w[:, 10288:10320], w[:, 10320:14416]
    out = jnp.zeros((w.shape[0], QKV0), BF16)
    for off, part in ((Z0, z), (XS0, xs), (MG0, mg), (BC0, bc), (SM0, gates), (SM0 + GATE_COLS, dt)):
        out = lax.dynamic_update_slice(out, part, (0, off))
    return out, w


def _layer(x2, attn_norm, w_in, gate_bias, cmp_pos, cmp_w1, cmp_b1, cmp_w2, conv_w, conv_b, dt_bias, a_log,
           d_skip, ssm_norm, w_bn, w_bs, w_out, ffn_norm, w_q, sub_keys, peer_u, peer_v, out_norm):
    proj, qkv = _stage_inproj(x2, attn_norm, *_cat_in_weights(w_in))
    kvc = _stage_compress(qkv, cmp_pos, cmp_w1, cmp_b1, cmp_w2)
    o_nsa = _stage_attention(qkv, kvc, proj)
    o_ssm = _stage_mamba(proj, conv_w, conv_b, dt_bias, a_log, d_skip, ssm_norm)
    x1, h2 = _stage_merge(o_nsa, o_ssm, proj, gate_bias, x2, w_bn, w_bs, w_out, ffn_norm)
    rk, nb, e1, e2 = _stage_peer_scores(h2, w_q, sub_keys)
    return _stage_peer_dense(h2, peer_u.astype(BF16), peer_v.astype(BF16), rk, nb, e1, e2, x1, out_norm)


def kernel(x, attn_norm, w_in, gate_bias, nsa_cmp_pos, nsa_cmp_w1, nsa_cmp_b1, nsa_cmp_w2, ssm_conv_w, ssm_conv_b,
           ssm_dt_bias, ssm_a_log, ssm_d, ssm_norm, w_branch_nsa, w_branch_ssm, w_out, ffn_norm, peer_w_q,
           peer_sub_keys, peer_u, peer_v, final_norm):
    b, s, d = x.shape
    assert b == 1 and d == D_MODEL and s % 1024 == 0
    depth = attn_norm.shape[0]
    x2 = x.reshape(s, d)
    for l in range(depth):
        x2 = _layer(x2, attn_norm[l], w_in[l], gate_bias[l], nsa_cmp_pos[l], nsa_cmp_w1[l], nsa_cmp_b1[l],
                    nsa_cmp_w2[l], ssm_conv_w[l], ssm_conv_b[l], ssm_dt_bias[l], ssm_a_log[l], ssm_d[l],
                    ssm_norm[l], w_branch_nsa[l], w_branch_ssm[l], w_out[l], ffn_norm[l], peer_w_q[l],
                    peer_sub_keys[l], peer_u[l], peer_v[l], final_norm if l + 1 == depth else None)
    return x2.reshape(b, s, d)
```
